```python
import math
import jax, jax.numpy as jnp
from jax import lax
import numpy as np

D_MODEL = 1024
BATCH = 4
SEQ = 4096
DEPTH = 4

N_MIXERS = 2
N_RG_LAYERS = (DEPTH + 1) // 2
N_AT_LAYERS = DEPTH // 2
NORM_EPS = 1e-6

LRU_WIDTH = D_MODEL
LRU_N_BLOCKS = 4
LRU_BLOCK_W = LRU_WIDTH // LRU_N_BLOCKS
CONV_WIDTH = 4
LRU_C = 8.0
A_MIN, A_MAX = 0.9, 0.999

ATT_WIDTH = D_MODEL
N_HEADS = 8
HEAD_DIM = ATT_WIDTH // N_HEADS
ROT_DIM = HEAD_DIM // 4
ROPE_THETA = 500000.0
MOBA_BLOCK = 256
MOBA_TOPK = 3
Q_CHUNK = 16

kernel_name = "hybrid_rglru_moba_gated"


def rms_norm(x, g):
    xf = x.astype(jnp.float32)
    y = xf * lax.rsqrt(jnp.mean(xf * xf, axis=-1, keepdims=True) + NORM_EPS)
    return (y * g.astype(jnp.float32)).astype(x.dtype)


def causal_depthwise_conv(x, w, b):
    width = x.shape[-1]
    y = lax.conv_general_dilated(
        x, w[:, None, :].astype(x.dtype), window_strides=(1,),
        padding=[(CONV_WIDTH - 1, 0)], dimension_numbers=("NWC", "WIO", "NWC"),
        feature_group_count=width)
    return y + b.astype(x.dtype)


def block_diag_linear(x, w, b):
    bsz, s, _ = x.shape
    xb = x.reshape(bsz, s, LRU_N_BLOCKS, LRU_BLOCK_W)
    y = jnp.einsum("bsnd,nde->bsne", xb, w.astype(x.dtype)) + b.astype(x.dtype)
    return y.reshape(bsz, s, LRU_WIDTH)


def _linear_scan_op(c1, c2):
    a1, b1 = c1
    a2, b2 = c2
    return a1 * a2, a2 * b1 + b2


def rg_lru(x, w_r, b_r, w_i, b_i, lam):
    r = jax.nn.sigmoid(block_diag_linear(x, w_r, b_r).astype(jnp.float32))
    i = jax.nn.sigmoid(block_diag_linear(x, w_i, b_i).astype(jnp.float32))
    log_a = LRU_C * r * jax.nn.log_sigmoid(lam.astype(jnp.float32))
    a = jnp.exp(log_a)
    mult = jnp.sqrt(jnp.maximum(1.0 - jnp.exp(2.0 * log_a), 0.0))
    u = mult * (i * x.astype(jnp.float32))
    _, h = lax.associative_scan(_linear_scan_op, (a, u), axis=1)
    return h.astype(x.dtype)


def recurrent_mixer(h, w_in, conv_w, conv_b, w_r, b_r, w_i, b_i, lam, w_out):
    proj = h @ w_in.astype(h.dtype)
    xb, gate = proj[..., :LRU_WIDTH], proj[..., LRU_WIDTH:]
    xb = causal_depthwise_conv(xb, conv_w, conv_b)
    y = rg_lru(xb, w_r, b_r, w_i, b_i, lam)
    return (y * jax.nn.silu(gate)) @ w_out.astype(h.dtype)


def rotary_tables(s):
    pos = jnp.arange(s, dtype=jnp.float32)
    inv_freq = ROPE_THETA ** (-jnp.arange(0, ROT_DIM, 2, dtype=jnp.float32) / ROT_DIM)
    ang = pos[:, None] * inv_freq[None, :]
    return jnp.cos(ang), jnp.sin(ang)


def apply_partial_rotary(x, cos, sin):
    half = ROT_DIM // 2
    x1, x2, xp = x[..., :half], x[..., half:ROT_DIM], x[..., ROT_DIM:]
    c, s = cos.astype(x.dtype), sin.astype(x.dtype)
    return jnp.concatenate([x1 * c - x2 * s, x2 * c + x1 * s, xp], axis=-1)


def moba_attention(q, k, v):
    bsz, nh, s, hd = q.shape
    scale = 1.0 / math.sqrt(hd)
    n_blk = -(-s // MOBA_BLOCK)
    pad = n_blk * MOBA_BLOCK - s
    k_p = jnp.pad(k, ((0, 0), (0, 0), (0, pad), (0, 0)))
    v_p = jnp.pad(v, ((0, 0), (0, 0), (0, pad), (0, 0)))
    k_blocks = k_p.reshape(bsz, nh, n_blk, MOBA_BLOCK, hd)
    v_blocks = v_p.reshape(bsz, nh, n_blk, MOBA_BLOCK, hd)

    k_mean = jnp.mean(k_blocks.astype(jnp.float32), axis=3)
    pos = jnp.arange(s)
    q_blk = pos // MOBA_BLOCK
    gate = jnp.einsum("bhsd,bhnd->bhsn", q.astype(jnp.float32), k_mean)
    past = jnp.arange(n_blk)[None, :] < q_blk[:, None]
    gate = jnp.where(past[None, None], gate, -jnp.inf)
    k_top = min(MOBA_TOPK, n_blk)
    _, sel = lax.top_k(gate, k_top)
    slot_valid = jnp.arange(k_top)[None, :] < q_blk[:, None]

    b_idx = jnp.arange(bsz)[:, None, None, None]
    h_idx = jnp.arange(nh)[None, :, None, None]
    n_chunks = s // Q_CHUNK

    def chunk(c):
        start = c * Q_CHUNK
        qc = lax.dynamic_slice_in_dim(q, start, Q_CHUNK, axis=2)
        sel_c = lax.dynamic_slice_in_dim(sel, start, Q_CHUNK, axis=2)
        valid_c = lax.dynamic_slice_in_dim(slot_valid, start, Q_CHUNK, axis=0)
        qpos = start + jnp.arange(Q_CHUNK)
        blk = start // MOBA_BLOCK
        own_k = lax.dynamic_slice_in_dim(k_p, blk * MOBA_BLOCK, MOBA_BLOCK, axis=2)
        own_v = lax.dynamic_slice_in_dim(v_p, blk * MOBA_BLOCK, MOBA_BLOCK, axis=2)
        k_sel = k_blocks[b_idx, h_idx, sel_c]
        v_sel = v_blocks[b_idx, h_idx, sel_c]
        s_sel = jnp.einsum("bhqd,bhqnkd->bhqnk", qc, k_sel).astype(jnp.float32) * scale
        s_sel = jnp.where(valid_c[None, None, :, :, None], s_sel, -jnp.inf)
        s_own = jnp.einsum("bhqd,bhkd->bhqk", qc, own_k).astype(jnp.float32) * scale
        key_pos = blk * MOBA_BLOCK + jnp.arange(MOBA_BLOCK)
        s_own = jnp.where((key_pos[None, :] <= qpos[:, None])[None, None], s_own, -jnp.inf)
        logits = jnp.concatenate(
            [s_sel.reshape(bsz, nh, Q_CHUNK, k_top * MOBA_BLOCK), s_own], axis=-1)
        p = jax.nn.softmax(logits, axis=-1).astype(v.dtype)
        p_sel = p[..., :k_top * MOBA_BLOCK].reshape(bsz, nh, Q_CHUNK, k_top, MOBA_BLOCK)
        p_own = p[..., k_top * MOBA_BLOCK:]
        return (jnp.einsum("bhqnk,bhqnkd->bhqd", p_sel, v_sel)
                + jnp.einsum("bhqk,bhkd->bhqd", p_own, own_v))

    out = lax.map(chunk, jnp.arange(n_chunks))
    return jnp.transpose(out, (1, 2, 0, 3, 4)).reshape(bsz, nh, s, hd)


def attention_mixer(h, w_in, w_out, cos, sin):
    bsz, s, _ = h.shape
    proj = h @ w_in.astype(h.dtype)
    q, k, v, gate = jnp.split(proj, 4, axis=-1)

    def heads(t):
        return t.reshape(bsz, s, N_HEADS, HEAD_DIM).transpose(0, 2, 1, 3)

    q = apply_partial_rotary(heads(q), cos, sin)
    k = apply_partial_rotary(heads(k), cos, sin)
    o = moba_attention(q, k, heads(v))
    o = o.transpose(0, 2, 1, 3).reshape(bsz, s, ATT_WIDTH)
    return (o * jax.nn.silu(gate)) @ w_out.astype(h.dtype)


def setup_inputs(seed: int = 0) -> dict:
    key = jax.random.key(seed)
    ks = jax.random.split(key, 20)
    na, nb = N_RG_LAYERS, N_AT_LAYERS
    f32 = jnp.float32

    def nrm(k, shape, fan_in):
        return jax.random.normal(k, shape, f32) * (fan_in ** -0.5)

    a_c = jax.random.uniform(ks[9], (na, LRU_WIDTH), f32, A_MIN, A_MAX)
    a0 = a_c ** (1.0 / LRU_C)
    lam = jnp.log(a0) - jnp.log1p(-a0)
    return {
        "x": jax.random.normal(ks[0], (BATCH, SEQ, D_MODEL), f32),
        "rg_norm": 1.0 + 0.02 * jax.random.normal(ks[1], (na, D_MODEL), f32),
        "rg_w_in": nrm(ks[2], (na, D_MODEL, 2 * LRU_WIDTH), D_MODEL),
        "rg_conv_w": nrm(ks[3], (na, CONV_WIDTH, LRU_WIDTH), CONV_WIDTH),
        "rg_conv_b": 0.01 * jax.random.normal(ks[4], (na, LRU_WIDTH), f32),
        "rg_w_rgate": nrm(ks[5], (na, LRU_N_BLOCKS, LRU_BLOCK_W, LRU_BLOCK_W), LRU_BLOCK_W),
        "rg_b_rgate": 0.01 * jax.random.normal(ks[6], (na, LRU_N_BLOCKS, LRU_BLOCK_W), f32),
        "rg_w_igate": nrm(ks[7], (na, LRU_N_BLOCKS, LRU_BLOCK_W, LRU_BLOCK_W), LRU_BLOCK_W),
        "rg_b_igate": 0.01 * jax.random.normal(ks[8], (na, LRU_N_BLOCKS, LRU_BLOCK_W), f32),
        "rg_lambda": lam,
        "rg_w_out": nrm(ks[10], (na, LRU_WIDTH, D_MODEL), LRU_WIDTH),
        "at_norm": 1.0 + 0.02 * jax.random.normal(ks[11], (nb, D_MODEL), f32),
        "at_w_in": nrm(ks[12], (nb, D_MODEL, 4 * ATT_WIDTH), D_MODEL),
        "at_w_out": nrm(ks[13], (nb, ATT_WIDTH, D_MODEL), ATT_WIDTH),
        "final_norm": 1.0 + 0.02 * jax.random.normal(ks[14], (D_MODEL,), f32),
    }


def reference(x, rg_norm, rg_w_in, rg_conv_w, rg_conv_b, rg_w_rgate, rg_b_rgate,
              rg_w_igate, rg_b_igate, rg_lambda, rg_w_out, at_norm, at_w_in, at_w_out,
              final_norm):
    cos, sin = rotary_tables(x.shape[1])
    for i in range(DEPTH):
        j = i // N_MIXERS
        if i % N_MIXERS == 0:
            h = rms_norm(x, rg_norm[j])
            x = x + recurrent_mixer(h, rg_w_in[j], rg_conv_w[j], rg_conv_b[j],
                                    rg_w_rgate[j], rg_b_rgate[j], rg_w_igate[j],
                                    rg_b_igate[j], rg_lambda[j], rg_w_out[j])
        else:
            h = rms_norm(x, at_norm[j])
            x = x + attention_mixer(h, at_w_in[j], at_w_out[j], cos, sin)
    return rms_norm(x, final_norm)
```

```python
import functools
import math

import jax
import jax.numpy as jnp
from jax import lax
from jax.experimental import pallas as pl
from jax.experimental.pallas import tpu as pltpu

NORM_EPS = 1e-6
LRU_N_BLOCKS = 4
CONV_WIDTH = 4
LRU_C = 8.0
N_HEADS = 8
HEAD_DIM = 128
ROT_DIM = HEAD_DIM // 4
ROT_HALF = ROT_DIM // 2
ROPE_THETA = 500000.0
MOBA_BLOCK = 256
MOBA_TOPK = 3

SUBLANES = 8
LANES = 128
TILE = MOBA_BLOCK
MASK_BIAS = -(2.0 ** 100)
VMEM_LIMIT = 56 * 1024 * 1024

F32 = jnp.float32
BF16 = jnp.bfloat16


def _rms_norm(x, g):
    ms = jnp.mean(x * x, axis=-1, keepdims=True)
    return x * lax.rsqrt(ms + NORM_EPS) * g


def _dot(a, b):
    return jnp.dot(a, b, preferred_element_type=F32)


def _dot_nt(a, b):
    return lax.dot_general(a, b, (((1,), (1,)), ((), ())), preferred_element_type=F32)


def _split_bf16(x):
    hi = x.astype(BF16)
    lo = (x - hi.astype(F32)).astype(BF16)
    return hi, lo


def _rg_kernel(x_ref, g_ref, win_ref, cw_ref, cb_ref, wr_ref, br_ref, wi_ref, bi_ref,
               lam_ref, wout_ref, o_ref, xbuf, hcar):
    t = pl.program_id(1)
    width = xbuf.shape[1]
    bw = width // LRU_N_BLOCKS

    @pl.when(t == 0)
    def _():
        xbuf[0:SUBLANES, :] = jnp.zeros((SUBLANES, width), F32)
        hcar[...] = jnp.zeros_like(hcar)

    x = x_ref[0]
    h = _rms_norm(x, g_ref[...]).astype(BF16)
    proj = _dot(h, win_ref[...])
    xb = proj[:, :width]
    gate = proj[:, width:]

    xbuf[SUBLANES:SUBLANES + TILE, :] = xb
    cw = cw_ref[...]
    conv = cb_ref[...] + cw[CONV_WIDTH - 1:CONV_WIDTH, :] * xb
    for back in range(1, CONV_WIDTH):
        tap = CONV_WIDTH - 1 - back
        conv = conv + cw[tap:tap + 1, :] * xbuf[SUBLANES - back:SUBLANES - back + TILE, :]
    xbuf[0:SUBLANES, :] = xbuf[TILE:TILE + SUBLANES, :]

    c16 = conv.astype(BF16)
    r_pre = jnp.concatenate(
        [_dot(c16[:, n * bw:(n + 1) * bw], wr_ref[n]) for n in range(LRU_N_BLOCKS)], axis=1)
    i_pre = jnp.concatenate(
        [_dot(c16[:, n * bw:(n + 1) * bw], wi_ref[n]) for n in range(LRU_N_BLOCKS)], axis=1)
    r = jax.nn.sigmoid(r_pre + br_ref[...])
    ig = jax.nn.sigmoid(i_pre + bi_ref[...])
    lam = lam_ref[...]
    log_sig = jnp.minimum(lam, 0.0) - jnp.log1p(jnp.exp(-jnp.abs(lam)))
    log_a = LRU_C * r * log_sig
    a = jnp.exp(log_a)
    mult = jnp.sqrt(jnp.maximum(1.0 - jnp.exp(2.0 * log_a), 0.0))
    u = mult * (ig * conv)

    groups = TILE // SUBLANES
    a3 = a.reshape(groups, SUBLANES, width)
    u3 = u.reshape(groups, SUBLANES, width)
    row = lax.broadcasted_iota(jnp.int32, (1, SUBLANES, width), 1)
    shift = 1
    while shift < SUBLANES:
        keep = row >= shift
        a_sh = jnp.where(keep, pltpu.roll(a3, shift, 1), 1.0)
        u_sh = jnp.where(keep, pltpu.roll(u3, shift, 1), 0.0)
        u3 = a3 * u_sh + u3
        a3 = a3 * a_sh
        shift *= 2
    hc = hcar[0:1, :]
    ys = []
    for g in range(groups):
        hg = a3[g] * hc + u3[g]
        ys.append(hg)
        hc = hg[SUBLANES - 1:SUBLANES, :]
    hcar[0:1, :] = hc
    y = jnp.concatenate(ys, axis=0)

    act = (y * (gate * jax.nn.sigmoid(gate))).astype(BF16)
    o_ref[0] = x + _dot(act, wout_ref[...])


def _rg_layer(x, g, w_in, conv_w, conv_b, w_r, b_r, w_i, b_i, lam, w_out):
    bsz, s, d = x.shape
    width = w_out.shape[0]
    bw = width // LRU_N_BLOCKS
    const2 = lambda b, t: (0, 0)
    const3 = lambda b, t: (0, 0, 0)
    return pl.pallas_call(
        _rg_kernel,
        grid=(bsz, s // TILE),
        in_specs=[
            pl.BlockSpec((1, TILE, d), lambda b, t: (b, t, 0)),
            pl.BlockSpec((1, d), const2),
            pl.BlockSpec((d, 2 * width), const2),
            pl.BlockSpec((CONV_WIDTH, width), const2),
            pl.BlockSpec((1, width), const2),
            pl.BlockSpec((LRU_N_BLOCKS, bw, bw), const3),
            pl.BlockSpec((1, width), const2),
            pl.BlockSpec((LRU_N_BLOCKS, bw, bw), const3),
            pl.BlockSpec((1, width), const2),
            pl.BlockSpec((1, width), const2),
            pl.BlockSpec((width, d), const2),
        ],
        out_specs=pl.BlockSpec((1, TILE, d), lambda b, t: (b, t, 0)),
        out_shape=jax.ShapeDtypeStruct(x.shape, F32),
        scratch_shapes=[
            pltpu.VMEM((TILE + SUBLANES, width), F32),
            pltpu.VMEM((SUBLANES, width), F32),
        ],
        compiler_params=pltpu.CompilerParams(
            dimension_semantics=("arbitrary", "arbitrary"), vmem_limit_bytes=VMEM_LIMIT),
        name="rg_layer",
    )(x, g.reshape(1, d), w_in.astype(BF16), conv_w, conv_b.reshape(1, width),
      w_r.astype(BF16), b_r.reshape(1, width), w_i.astype(BF16), b_i.reshape(1, width),
      lam.reshape(1, width), w_out.astype(BF16))


def _at_in_kernel(x_ref, g_ref, win_ref, cos_ref, sa_ref, sb_ref,
                  q_ref, k_ref, v_ref, gate_ref, selm_ref, mt):
    i = pl.program_id(1)
    aw = N_HEADS * HEAD_DIM

    @pl.when(i == 0)
    def _():
        mt[...] = jnp.zeros_like(mt)

    x = x_ref[0]
    h = _rms_norm(x, g_ref[...]).astype(BF16)
    proj = _dot(h, win_ref[...])

    cos = cos_ref[...]
    sa = sa_ref[...]
    sb = sb_ref[...]

    def rotary(z):
        return (z * cos + pltpu.roll(z, HEAD_DIM - ROT_HALF, 1) * sa
                + pltpu.roll(z, ROT_HALF, 1) * sb)

    qf = jnp.concatenate(
        [rotary(proj[:, hh * HEAD_DIM:(hh + 1) * HEAD_DIM]) for hh in range(N_HEADS)], axis=1)
    kf = jnp.concatenate(
        [rotary(proj[:, aw + hh * HEAD_DIM:aw + (hh + 1) * HEAD_DIM]) for hh in range(N_HEADS)],
        axis=1)

    q_hi, q_lo = _split_bf16(qf)
    m_hi, m_lo = _split_bf16(mt[...])
    score = _dot_nt(q_hi, m_hi) + (_dot_nt(q_hi, m_lo) + _dot_nt(q_lo, m_hi))
    lane = lax.broadcasted_iota(jnp.int32, (TILE, LANES), 1)
    blk = lane // N_HEADS
    valid = blk < i
    score = jnp.where(valid, score, -jnp.inf)
    beaten = jnp.zeros((TILE, LANES), jnp.int32)
    for d in range(1, LANES // N_HEADS):
        other = pltpu.roll(score, d * N_HEADS, 1)
        wins = (other > score) | ((other == score) & (blk >= d))
        beaten = beaten + wins.astype(jnp.int32)
    sel = valid & (beaten < MOBA_TOPK)
    selm_ref[0] = jnp.where(sel, 0.0, MASK_BIAS).astype(BF16)

    kmean = jnp.mean(kf, axis=0, keepdims=True)
    head_of_col = lax.broadcasted_iota(jnp.int32, (N_HEADS, aw), 1) // HEAD_DIM
    head_of_row = lax.broadcasted_iota(jnp.int32, (N_HEADS, aw), 0)
    mt[pl.ds(pl.multiple_of(i * N_HEADS, N_HEADS), N_HEADS), :] = jnp.where(
        head_of_col == head_of_row, jnp.broadcast_to(kmean, (N_HEADS, aw)), 0.0)

    q_ref[0] = (qf * (1.0 / math.sqrt(HEAD_DIM))).astype(BF16)
    k_ref[0] = kf.astype(BF16)
    v_ref[0] = proj[:, 2 * aw:3 * aw].astype(BF16)
    gate_ref[0] = proj[:, 3 * aw:]


def _at_in(x, g, w_in, cos_t, sa_t, sb_t):
    bsz, s, d = x.shape
    aw = N_HEADS * HEAD_DIM
    const2 = lambda b, t: (0, 0)
    row_blk = lambda b, t: (b, t, 0)
    return pl.pallas_call(
        _at_in_kernel,
        grid=(bsz, s // TILE),
        in_specs=[
            pl.BlockSpec((1, TILE, d), row_blk),
            pl.BlockSpec((1, d), const2),
            pl.BlockSpec((d, 4 * aw), const2),
            pl.BlockSpec((TILE, HEAD_DIM), lambda b, t: (t, 0)),
            pl.BlockSpec((TILE, HEAD_DIM), lambda b, t: (t, 0)),
            pl.BlockSpec((TILE, HEAD_DIM), lambda b, t: (t, 0)),
        ],
        out_specs=[
            pl.BlockSpec((1, TILE, aw), row_blk),
            pl.BlockSpec((1, TILE, aw), row_blk),
            pl.BlockSpec((1, TILE, aw), row_blk),
            pl.BlockSpec((1, TILE, aw), row_blk),
            pl.BlockSpec((1, TILE, LANES), row_blk),
        ],
        out_shape=[
            jax.ShapeDtypeStruct((bsz, s, aw), BF16),
            jax.ShapeDtypeStruct((bsz, s, aw), BF16),
            jax.ShapeDtypeStruct((bsz, s, aw), BF16),
            jax.ShapeDtypeStruct((bsz, s, aw), F32),
            jax.ShapeDtypeStruct((bsz, s, LANES), BF16),
        ],
        scratch_shapes=[pltpu.VMEM((LANES, aw), F32)],
        compiler_params=pltpu.CompilerParams(
            dimension_semantics=("arbitrary", "arbitrary"), vmem_limit_bytes=VMEM_LIMIT),
        name="at_in",
    )(x, g.reshape(1, d), w_in.astype(BF16), cos_t, sa_t, sb_t)


def _attn_kernel(q_ref, selm_ref, k_ref, v_ref, gate_ref, o_ref):
    hh = pl.program_id(1)
    i = pl.program_id(2)
    q = q_ref[0]
    own = pl.multiple_of(i * TILE, TILE)

    s = _dot_nt(q, k_ref[0, pl.ds(own, TILE), :])
    rows = lax.broadcasted_iota(jnp.int32, (TILE, TILE), 0)
    cols = lax.broadcasted_iota(jnp.int32, (TILE, TILE), 1)
    s = jnp.where(cols <= rows, s, -jnp.inf)
    m0 = jnp.max(s, axis=1, keepdims=True)
    p = jnp.exp(s - m0)
    l0 = jnp.sum(p, axis=1, keepdims=True)
    acc0 = _dot(p.astype(BF16), v_ref[0, pl.ds(own, TILE), :])

    qa = jnp.concatenate([q, selm_ref[0]], axis=1)
    lane = lax.broadcasted_iota(jnp.int32, (TILE, LANES), 1)

    def body(j, carry):
        m, l, acc = carry
        start = pl.multiple_of(j * TILE, TILE)
        kj = k_ref[0, pl.ds(start, TILE), :]
        onehot = (lane == j * N_HEADS + hh).astype(BF16)
        sj = _dot_nt(qa, jnp.concatenate([kj, onehot], axis=1))
        m_new = jnp.maximum(m, jnp.max(sj, axis=1, keepdims=True))
        alpha = jnp.exp(m - m_new)
        pj = jnp.exp(sj - m_new)
        l = alpha * l + jnp.sum(pj, axis=1, keepdims=True)
        acc = alpha * acc + _dot(pj.astype(BF16), v_ref[0, pl.ds(start, TILE), :])
        return m_new, l, acc

    _, l, acc = lax.fori_loop(0, i, body, (m0, l0, acc0))
    gate = gate_ref[0]
    o_ref[0] = ((acc / l) * (gate * jax.nn.sigmoid(gate))).astype(BF16)


def _attention(q, selm, k, v, gate):
    bsz, s, aw = q.shape
    q_blk = lambda b, h, i: (b, i, h)
    kv_blk = lambda b, h, i: (b, 0, h)
    return pl.pallas_call(
        _attn_kernel,
        grid=(bsz, N_HEADS, s // TILE),
        in_specs=[
            pl.BlockSpec((1, TILE, HEAD_DIM), q_blk),
            pl.BlockSpec((1, TILE, LANES), lambda b, h, i: (b, i, 0)),
            pl.BlockSpec((1, s, HEAD_DIM), kv_blk),
            pl.BlockSpec((1, s, HEAD_DIM), kv_blk),
            pl.BlockSpec((1, TILE, HEAD_DIM), q_blk),
        ],
        out_specs=pl.BlockSpec((1, TILE, HEAD_DIM), q_blk),
        out_shape=jax.ShapeDtypeStruct((bsz, s, aw), BF16),
        compiler_params=pltpu.CompilerParams(
            dimension_semantics=("arbitrary", "arbitrary", "arbitrary"),
            vmem_limit_bytes=VMEM_LIMIT),
        name="moba_attn",
    )(q, selm, k, v, gate)


def _at_out_kernel(a_ref, x_ref, w_ref, g_ref, o_ref, *, final):
    y = x_ref[0] + _dot(a_ref[0], w_ref[...])
    if final:
        y = _rms_norm(y, g_ref[...])
    o_ref[0] = y


def _at_out(act, x, w_out, g_final, final):
    bsz, s, d = x.shape
    aw = act.shape[-1]
    row_blk = lambda b, t: (b, t, 0)
    const2 = lambda b, t: (0, 0)
    return pl.pallas_call(
        functools.partial(_at_out_kernel, final=final),
        grid=(bsz, s // TILE),
        in_specs=[
            pl.BlockSpec((1, TILE, aw), row_blk),
            pl.BlockSpec((1, TILE, d), row_blk),
            pl.BlockSpec((aw, d), const2),
            pl.BlockSpec((1, d), const2),
        ],
        out_specs=pl.BlockSpec((1, TILE, d), row_blk),
        out_shape=jax.ShapeDtypeStruct(x.shape, F32),
        compiler_params=pltpu.CompilerParams(
            dimension_semantics=("arbitrary", "arbitrary"), vmem_limit_bytes=VMEM_LIMIT),
        name="at_out_final" if final else "at_out",
    )(act, x, w_out.astype(BF16), g_final.reshape(1, d))


def _rotary_tables(s):
    pos = jnp.arange(s, dtype=F32)
    inv_freq = ROPE_THETA ** (-jnp.arange(0, ROT_DIM, 2, dtype=F32) / ROT_DIM)
    ang = pos[:, None] * inv_freq[None, :]
    cos, sin = jnp.cos(ang), jnp.sin(ang)
    zeros = jnp.zeros((s, HEAD_DIM - ROT_DIM), F32)
    half0 = jnp.zeros((s, ROT_HALF), F32)
    cos_t = jnp.concatenate([cos, cos, jnp.ones((s, HEAD_DIM - ROT_DIM), F32)], axis=1)
    sa_t = jnp.concatenate([-sin, half0, zeros], axis=1)
    sb_t = jnp.concatenate([half0, sin, zeros], axis=1)
    return cos_t, sa_t, sb_t


def kernel(x, rg_norm, rg_w_in, rg_conv_w, rg_conv_b, rg_w_rgate, rg_b_rgate, rg_w_igate,
           rg_b_igate, rg_lambda, rg_w_out, at_norm, at_w_in, at_w_out, final_norm):
    bsz, s, d = x.shape
    assert s % TILE == 0 and s // TILE * N_HEADS == LANES
    depth = rg_norm.shape[0] + at_norm.shape[0]
    cos_t, sa_t, sb_t = _rotary_tables(s)
    for layer in range(depth):
        j = layer // 2
        if layer % 2 == 0:
            x = _rg_layer(x, rg_norm[j], rg_w_in[j], rg_conv_w[j], rg_conv_b[j],
                          rg_w_rgate[j], rg_b_rgate[j], rg_w_igate[j], rg_b_igate[j],
                          rg_lambda[j], rg_w_out[j])
        else:
            q, k, v, gate, selm = _at_in(x, at_norm[j], at_w_in[j], cos_t, sa_t, sb_t)
            act = _attention(q, selm, k, v, gate)
            x = _at_out(act, x, at_w_out[j], final_norm, final=(layer == depth - 1))
    return x
```

```python
import functools
import math

import jax
import jax.numpy as jnp
from jax import lax
from jax.experimental import pallas as pl
from jax.experimental.pallas import tpu as pltpu

NORM_EPS = 1e-6
LRU_N_BLOCKS = 4
CONV_WIDTH = 4
LRU_C = 8.0
N_HEADS = 8
HEAD_DIM = 128
ROT_DIM = HEAD_DIM // 4
ROT_HALF = ROT_DIM // 2
ROPE_THETA = 500000.0
MOBA_BLOCK = 256
MOBA_TOPK = 3

SUBLANES = 8
LANES = 128
TILE = MOBA_BLOCK
MASK_BIAS = -(2.0 ** 100)
VMEM_LIMIT = 56 * 1024 * 1024
LOG2_E = 1.4426950408889634
KV_BLOCKS_PER_STEP = 2

F32 = jnp.float32
BF16 = jnp.bfloat16


def _rms_norm(x, g):
    ms = jnp.mean(x * x, axis=-1, keepdims=True)
    return x * lax.rsqrt(ms + NORM_EPS) * g


def _dot(a, b):
    return jnp.dot(a, b, preferred_element_type=F32)


def _dot_nt(a, b):
    return lax.dot_general(a, b, (((1,), (1,)), ((), ())), preferred_element_type=F32)


def _split_bf16(x):
    hi = x.astype(BF16)
    lo = (x - hi.astype(F32)).astype(BF16)
    return hi, lo


def _rg_kernel(x_ref, g_ref, win_ref, cw_ref, cb_ref, wr_ref, br_ref, wi_ref, bi_ref,
               lam_ref, wout_ref, o_ref, xbuf, hcar):
    t = pl.program_id(1)
    width = xbuf.shape[1]
    bw = width // LRU_N_BLOCKS

    @pl.when(t == 0)
    def _():
        xbuf[0:SUBLANES, :] = jnp.zeros((SUBLANES, width), F32)
        hcar[...] = jnp.zeros_like(hcar)

    x = x_ref[0]
    h = _rms_norm(x, g_ref[...]).astype(BF16)
    proj = _dot(h, win_ref[...])
    xb = proj[:, :width]
    gate = proj[:, width:]

    xbuf[SUBLANES:SUBLANES + TILE, :] = xb
    cw = cw_ref[...]
    conv = cb_ref[...] + cw[CONV_WIDTH - 1:CONV_WIDTH, :] * xb
    for back in range(1, CONV_WIDTH):
        tap = CONV_WIDTH - 1 - back
        conv = conv + cw[tap:tap + 1, :] * xbuf[SUBLANES - back:SUBLANES - back + TILE, :]
    xbuf[0:SUBLANES, :] = xbuf[TILE:TILE + SUBLANES, :]

    c16 = conv.astype(BF16)
    r_pre = jnp.concatenate(
        [_dot(c16[:, n * bw:(n + 1) * bw], wr_ref[n]) for n in range(LRU_N_BLOCKS)], axis=1)
    i_pre = jnp.concatenate(
        [_dot(c16[:, n * bw:(n + 1) * bw], wi_ref[n]) for n in range(LRU_N_BLOCKS)], axis=1)
    r = jax.nn.sigmoid(r_pre + br_ref[...])
    ig = jax.nn.sigmoid(i_pre + bi_ref[...])
    lam = lam_ref[...]
    log_sig = jnp.minimum(lam, 0.0) - jnp.log1p(jnp.exp(-jnp.abs(lam)))
    log_a = LRU_C * r * log_sig
    a = jnp.exp(log_a)
    mult = jnp.sqrt(jnp.maximum(1.0 - jnp.exp(2.0 * log_a), 0.0))
    u = mult * (ig * conv)

    groups = TILE // SUBLANES
    a3 = a.reshape(groups, SUBLANES, width)
    u3 = u.reshape(groups, SUBLANES, width)
    row = lax.broadcasted_iota(jnp.int32, (1, SUBLANES, width), 1)
    shift = 1
    while shift < SUBLANES:
        keep = row >= shift
        a_sh = jnp.where(keep, pltpu.roll(a3, shift, 1), 1.0)
        u_sh = jnp.where(keep, pltpu.roll(u3, shift, 1), 0.0)
        u3 = a3 * u_sh + u3
        a3 = a3 * a_sh
        shift *= 2
    hc = hcar[0:1, :]
    ys = []
    for g in range(groups):
        hg = a3[g] * hc + u3[g]
        ys.append(hg)
        hc = hg[SUBLANES - 1:SUBLANES, :]
    hcar[0:1, :] = hc
    y = jnp.concatenate(ys, axis=0)

    act = (y * (gate * jax.nn.sigmoid(gate))).astype(BF16)
    o_ref[0] = x + _dot(act, wout_ref[...])


def _rg_layer(x, g, w_in, conv_w, conv_b, w_r, b_r, w_i, b_i, lam, w_out):
    bsz, s, d = x.shape
    width = w_out.shape[0]
    bw = width // LRU_N_BLOCKS
    const2 = lambda b, t: (0, 0)
    const3 = lambda b, t: (0, 0, 0)
    return pl.pallas_call(
        _rg_kernel,
        grid=(bsz, s // TILE),
        in_specs=[
            pl.BlockSpec((1, TILE, d), lambda b, t: (b, t, 0)),
            pl.BlockSpec((1, d), const2),
            pl.BlockSpec((d, 2 * width), const2),
            pl.BlockSpec((CONV_WIDTH, width), const2),
            pl.BlockSpec((1, width), const2),
            pl.BlockSpec((LRU_N_BLOCKS, bw, bw), const3),
            pl.BlockSpec((1, width), const2),
            pl.BlockSpec((LRU_N_BLOCKS, bw, bw), const3),
            pl.BlockSpec((1, width), const2),
            pl.BlockSpec((1, width), const2),
            pl.BlockSpec((width, d), const2),
        ],
        out_specs=pl.BlockSpec((1, TILE, d), lambda b, t: (b, t, 0)),
        out_shape=jax.ShapeDtypeStruct(x.shape, F32),
        scratch_shapes=[
            pltpu.VMEM((TILE + SUBLANES, width), F32),
            pltpu.VMEM((SUBLANES, width), F32),
        ],
        compiler_params=pltpu.CompilerParams(
            dimension_semantics=("arbitrary", "arbitrary"), vmem_limit_bytes=VMEM_LIMIT),
        name="rg_layer",
    )(x, g.reshape(1, d), w_in.astype(BF16), conv_w, conv_b.reshape(1, width),
      w_r.astype(BF16), b_r.reshape(1, width), w_i.astype(BF16), b_i.reshape(1, width),
      lam.reshape(1, width), w_out.astype(BF16))


def _at_in_kernel(x_ref, g_ref, win_ref, cos_ref, sa_ref, sb_ref,
                  q_ref, k_ref, v_ref, gate_ref, selm_ref, mt):
    i = pl.program_id(1)
    aw = N_HEADS * HEAD_DIM

    @pl.when(i == 0)
    def _():
        mt[...] = jnp.zeros_like(mt)

    x = x_ref[0]
    h = _rms_norm(x, g_ref[...]).astype(BF16)
    proj = _dot(h, win_ref[...])

    cos = cos_ref[...]
    sa = sa_ref[...]
    sb = sb_ref[...]

    def rotary(z):
        return (z * cos + pltpu.roll(z, HEAD_DIM - ROT_HALF, 1) * sa
                + pltpu.roll(z, ROT_HALF, 1) * sb)

    qf = jnp.concatenate(
        [rotary(proj[:, hh * HEAD_DIM:(hh + 1) * HEAD_DIM]) for hh in range(N_HEADS)], axis=1)
    kf = jnp.concatenate(
        [rotary(proj[:, aw + hh * HEAD_DIM:aw + (hh + 1) * HEAD_DIM]) for hh in range(N_HEADS)],
        axis=1)

    q_hi, q_lo = _split_bf16(qf)
    m_hi, m_lo = _split_bf16(mt[...])
    score = _dot_nt(q_hi, m_hi) + (_dot_nt(q_hi, m_lo) + _dot_nt(q_lo, m_hi))
    lane = lax.broadcasted_iota(jnp.int32, (TILE, LANES), 1)
    blk = lane // N_HEADS
    valid = blk < i
    score = jnp.where(valid, score, -jnp.inf)
    beaten = jnp.zeros((TILE, LANES), jnp.int32)
    for d in range(1, LANES // N_HEADS):
        other = pltpu.roll(score, d * N_HEADS, 1)
        wins = (other > score) | ((other == score) & (blk >= d))
        beaten = beaten + wins.astype(jnp.int32)
    sel = valid & (beaten < MOBA_TOPK)
    selm_ref[0] = jnp.where(sel, 0.0, MASK_BIAS).astype(BF16)

    kmean = jnp.mean(kf, axis=0, keepdims=True)
    head_of_col = lax.broadcasted_iota(jnp.int32, (N_HEADS, aw), 1) // HEAD_DIM
    head_of_row = lax.broadcasted_iota(jnp.int32, (N_HEADS, aw), 0)
    mt[pl.ds(pl.multiple_of(i * N_HEADS, N_HEADS), N_HEADS), :] = jnp.where(
        head_of_col == head_of_row, jnp.broadcast_to(kmean, (N_HEADS, aw)), 0.0)

    q_ref[0] = (qf * (LOG2_E / math.sqrt(HEAD_DIM))).astype(BF16)
    k_ref[0] = kf.astype(BF16)
    v_ref[0] = proj[:, 2 * aw:3 * aw].astype(BF16)
    gate_ref[0] = proj[:, 3 * aw:]


def _at_in(x, g, w_in, cos_t, sa_t, sb_t):
    bsz, s, d = x.shape
    aw = N_HEADS * HEAD_DIM
    const2 = lambda b, t: (0, 0)
    row_blk = lambda b, t: (b, t, 0)
    return pl.pallas_call(
        _at_in_kernel,
        grid=(bsz, s // TILE),
        in_specs=[
            pl.BlockSpec((1, TILE, d), row_blk),
            pl.BlockSpec((1, d), const2),
            pl.BlockSpec((d, 4 * aw), const2),
            pl.BlockSpec((TILE, HEAD_DIM), lambda b, t: (t, 0)),
            pl.BlockSpec((TILE, HEAD_DIM), lambda b, t: (t, 0)),
            pl.BlockSpec((TILE, HEAD_DIM), lambda b, t: (t, 0)),
        ],
        out_specs=[
            pl.BlockSpec((1, TILE, aw), row_blk),
            pl.BlockSpec((1, TILE, aw), row_blk),
            pl.BlockSpec((1, TILE, aw), row_blk),
            pl.BlockSpec((1, TILE, aw), row_blk),
            pl.BlockSpec((1, TILE, LANES), row_blk),
        ],
        out_shape=[
            jax.ShapeDtypeStruct((bsz, s, aw), BF16),
            jax.ShapeDtypeStruct((bsz, s, aw), BF16),
            jax.ShapeDtypeStruct((bsz, s, aw), BF16),
            jax.ShapeDtypeStruct((bsz, s, aw), F32),
            jax.ShapeDtypeStruct((bsz, s, LANES), BF16),
        ],
        scratch_shapes=[pltpu.VMEM((LANES, aw), F32)],
        compiler_params=pltpu.CompilerParams(
            dimension_semantics=("arbitrary", "arbitrary"), vmem_limit_bytes=VMEM_LIMIT),
        name="at_in",
    )(x, g.reshape(1, d), w_in.astype(BF16), cos_t, sa_t, sb_t)


def _attn_kernel(q_ref, selm_ref, k_ref, v_ref, gate_ref, o_ref, m_sc, acc_sc):
    i = pl.program_id(1)
    own = pl.multiple_of(i * TILE, TILE)
    rows = lax.broadcasted_iota(jnp.int32, (TILE, TILE), 0)
    cols = lax.broadcasted_iota(jnp.int32, (TILE, TILE), 1)
    causal = cols <= rows
    kv_step = KV_BLOCKS_PER_STEP * TILE

    ones_own = jnp.ones((TILE, LANES), BF16)
    for hh in range(N_HEADS):
        cs = slice(hh * HEAD_DIM, (hh + 1) * HEAD_DIM)
        s = _dot_nt(q_ref[0, :, cs], k_ref[0, pl.ds(own, TILE), cs])
        s = jnp.where(causal, s, -jnp.inf)
        m0 = jnp.max(s, axis=1, keepdims=True)
        p = jnp.exp2(s - m0)
        m_sc[hh] = jnp.broadcast_to(m0, (TILE, LANES))
        va = jnp.concatenate([v_ref[0, pl.ds(own, TILE), cs], ones_own], axis=1)
        acc_sc[hh] = _dot(p.astype(BF16), va)

    lane_row = lax.broadcasted_iota(jnp.int32, (1, LANES), 1)
    ones_step = jnp.ones((kv_step, LANES), BF16)

    def body(jj, carry):
        start = pl.multiple_of(jj * kv_step, kv_step)
        selm = selm_ref[0]
        for hh in range(N_HEADS):
            cs = slice(hh * HEAD_DIM, (hh + 1) * HEAD_DIM)
            qa = jnp.concatenate([q_ref[0, :, cs], selm], axis=1)
            onehot = jnp.concatenate([
                jnp.broadcast_to(
                    (lane_row == (jj * KV_BLOCKS_PER_STEP + c) * N_HEADS + hh).astype(BF16),
                    (TILE, LANES))
                for c in range(KV_BLOCKS_PER_STEP)], axis=0)
            ka = jnp.concatenate([k_ref[0, pl.ds(start, kv_step), cs], onehot], axis=1)
            s = _dot_nt(qa, ka)
            m_prev = m_sc[hh]
            m_new = jnp.maximum(m_prev, jnp.max(s, axis=1, keepdims=True))
            alpha = jnp.exp2(m_prev - m_new)
            p = jnp.concatenate(
                [jnp.exp2(s[:, c * LANES:(c + 1) * LANES] - m_new).astype(BF16)
                 for c in range(kv_step // LANES)], axis=1)
            va = jnp.concatenate([v_ref[0, pl.ds(start, kv_step), cs], ones_step], axis=1)
            pv = _dot(p, va)
            acc_sc[hh, :, :HEAD_DIM] = alpha * acc_sc[hh, :, :HEAD_DIM] + pv[:, :HEAD_DIM]
            acc_sc[hh, :, HEAD_DIM:] = alpha * acc_sc[hh, :, HEAD_DIM:] + pv[:, HEAD_DIM:]
            m_sc[hh] = m_new
        return carry

    lax.fori_loop(0, (i + KV_BLOCKS_PER_STEP - 1) // KV_BLOCKS_PER_STEP, body, 0)
    for hh in range(N_HEADS):
        cs = slice(hh * HEAD_DIM, (hh + 1) * HEAD_DIM)
        gate = gate_ref[0, :, cs]
        acc = acc_sc[hh]
        o_ref[0, :, cs] = ((acc[:, :HEAD_DIM] / acc[:, HEAD_DIM:])
                           * (gate * jax.nn.sigmoid(gate))).astype(BF16)


def _attention(q, selm, k, v, gate):
    bsz, s, aw = q.shape
    q_blk = lambda b, i: (b, i, 0)
    kv_blk = lambda b, i: (b, 0, 0)
    return pl.pallas_call(
        _attn_kernel,
        grid=(bsz, s // TILE),
        in_specs=[
            pl.BlockSpec((1, TILE, aw), q_blk),
            pl.BlockSpec((1, TILE, LANES), q_blk),
            pl.BlockSpec((1, s, aw), kv_blk),
            pl.BlockSpec((1, s, aw), kv_blk),
            pl.BlockSpec((1, TILE, aw), q_blk),
        ],
        out_specs=pl.BlockSpec((1, TILE, aw), q_blk),
        out_shape=jax.ShapeDtypeStruct((bsz, s, aw), BF16),
        scratch_shapes=[
            pltpu.VMEM((N_HEADS, TILE, LANES), F32),
            pltpu.VMEM((N_HEADS, TILE, HEAD_DIM + LANES), F32),
        ],
        compiler_params=pltpu.CompilerParams(
            dimension_semantics=("arbitrary", "arbitrary"), vmem_limit_bytes=VMEM_LIMIT),
        name="moba_attn",
    )(q, selm, k, v, gate)


def _at_out_kernel(a_ref, x_ref, w_ref, g_ref, o_ref, *, final):
    y = x_ref[0] + _dot(a_ref[0], w_ref[...])
    if final:
        y = _rms_norm(y, g_ref[...])
    o_ref[0] = y


def _at_out(act, x, w_out, g_final, final):
    bsz, s, d = x.shape
    aw = act.shape[-1]
    row_blk = lambda b, t: (b, t, 0)
    const2 = lambda b, t: (0, 0)
    return pl.pallas_call(
        functools.partial(_at_out_kernel, final=final),
        grid=(bsz, s // TILE),
        in_specs=[
            pl.BlockSpec((1, TILE, aw), row_blk),
            pl.BlockSpec((1, TILE, d), row_blk),
            pl.BlockSpec((aw, d), const2),
            pl.BlockSpec((1, d), const2),
        ],
        out_specs=pl.BlockSpec((1, TILE, d), row_blk),
        out_shape=jax.ShapeDtypeStruct(x.shape, F32),
        compiler_params=pltpu.CompilerParams(
            dimension_semantics=("arbitrary", "arbitrary"), vmem_limit_bytes=VMEM_LIMIT),
        name="at_out_final" if final else "at_out",
    )(act, x, w_out.astype(BF16), g_final.reshape(1, d))


def _rotary_tables(s):
    pos = jnp.arange(s, dtype=F32)
    inv_freq = ROPE_THETA ** (-jnp.arange(0, ROT_DIM, 2, dtype=F32) / ROT_DIM)
    ang = pos[:, None] * inv_freq[None, :]
    cos, sin = jnp.cos(ang), jnp.sin(ang)
    zeros = jnp.zeros((s, HEAD_DIM - ROT_DIM), F32)
    half0 = jnp.zeros((s, ROT_HALF), F32)
    cos_t = jnp.concatenate([cos, cos, jnp.ones((s, HEAD_DIM - ROT_DIM), F32)], axis=1)
    sa_t = jnp.concatenate([-sin, half0, zeros], axis=1)
    sb_t = jnp.concatenate([half0, sin, zeros], axis=1)
    return cos_t, sa_t, sb_t


def kernel(x, rg_norm, rg_w_in, rg_conv_w, rg_conv_b, rg_w_rgate, rg_b_rgate, rg_w_igate,
           rg_b_igate, rg_lambda, rg_w_out, at_norm, at_w_in, at_w_out, final_norm):
    bsz, s, d = x.shape
    assert s % TILE == 0 and s // TILE * N_HEADS == LANES
    depth = rg_norm.shape[0] + at_norm.shape[0]
    cos_t, sa_t, sb_t = _rotary_tables(s)
    for layer in range(depth):
        j = layer // 2
        if layer % 2 == 0:
            x = _rg_layer(x, rg_norm[j], rg_w_in[j], rg_conv_w[j], rg_conv_b[j],
                          rg_w_rgate[j], rg_b_rgate[j], rg_w_igate[j], rg_b_igate[j],
                          rg_lambda[j], rg_w_out[j])
        else:
            q, k, v, gate, selm = _at_in(x, at_norm[j], at_w_in[j], cos_t, sa_t, sb_t)
            act = _attention(q, selm, k, v, gate)
            x = _at_out(act, x, at_w_out[j], final_norm, final=(layer == depth - 1))
    return x
```

```python
import functools
import math

import jax
import jax.numpy as jnp
from jax import lax
from jax.experimental import pallas as pl
from jax.experimental.pallas import tpu as pltpu

NORM_EPS = 1e-6
LRU_N_BLOCKS = 4
CONV_WIDTH = 4
LRU_C = 8.0
N_HEADS = 8
HEAD_DIM = 128
ROT_DIM = HEAD_DIM // 4
ROT_HALF = ROT_DIM // 2
ROPE_THETA = 500000.0
MOBA_BLOCK = 256
MOBA_TOPK = 3

SUBLANES = 8
LANES = 128
TILE = MOBA_BLOCK
SEG = TILE // SUBLANES
SEG_PITCH = SEG + 4
MASK_BIAS = -(2.0 ** 100)
VMEM_LIMIT = 56 * 1024 * 1024
LOG2_E = 1.4426950408889634
KV_BLOCKS_PER_STEP = 2

F32 = jnp.float32
BF16 = jnp.bfloat16


def _rms_norm(x, g):
    ms = jnp.mean(x * x, axis=-1, keepdims=True)
    return x * lax.rsqrt(ms + NORM_EPS) * g


def _dot(a, b):
    return jnp.dot(a, b, preferred_element_type=F32)


def _dot_nt(a, b):
    return lax.dot_general(a, b, (((1,), (1,)), ((), ())), preferred_element_type=F32)


def _split_bf16(x):
    hi = x.astype(BF16)
    lo = (x - hi.astype(F32)).astype(BF16)
    return hi, lo


def _rg_kernel(x_ref, g_ref, win_ref, cw_ref, cb_ref, wr_ref, br_ref, wi_ref, bi_ref,
               lam_ref, wout_ref, o_ref,
               win_s, wr_s, wi_s, wout_s, xb_sc, a_sc, u_sc, ap_sc, h_sc, act_sc, hcar):
    t = pl.program_id(1)
    n_chunk = xb_sc.shape[0]
    width = n_chunk * LANES
    bw = width // LRU_N_BLOCKS

    @pl.when((pl.program_id(0) == 0) & (t == 0))
    def _():
        win_s[...] = win_ref[...]
        wr_s[...] = wr_ref[...]
        wi_s[...] = wi_ref[...]
        wout_s[...] = wout_ref[...]

    @pl.when(t == 0)
    def _():
        xb_sc[:, 0:SUBLANES, :] = jnp.zeros((n_chunk, SUBLANES, LANES), F32)
        hcar[...] = jnp.zeros_like(hcar)

    x = x_ref[0]
    h = _rms_norm(x, g_ref[...]).astype(BF16)
    proj = _dot(h, win_s[...])
    gate = proj[:, width:]

    cw = cw_ref[...]
    cb = cb_ref[...]
    conv_chunks = []
    for c in range(n_chunk):
        cs = slice(c * LANES, (c + 1) * LANES)
        xb_sc[c, SUBLANES:SUBLANES + TILE, :] = proj[:, cs]
        acc = cb[:, cs]
        for tap in range(CONV_WIDTH):
            back = CONV_WIDTH - 1 - tap
            acc = acc + cw[tap:tap + 1, cs] * xb_sc[c, SUBLANES - back:SUBLANES - back + TILE, :]
        conv_chunks.append(acc)
        xb_sc[c, 0:SUBLANES, :] = xb_sc[c, TILE:TILE + SUBLANES, :]
    conv = jnp.concatenate(conv_chunks, axis=1)

    c16 = conv.astype(BF16)
    r_pre = jnp.concatenate(
        [_dot(c16[:, n * bw:(n + 1) * bw], wr_s[n]) for n in range(LRU_N_BLOCKS)], axis=1)
    i_pre = jnp.concatenate(
        [_dot(c16[:, n * bw:(n + 1) * bw], wi_s[n]) for n in range(LRU_N_BLOCKS)], axis=1)
    r = jax.nn.sigmoid(r_pre + br_ref[...])
    ig = jax.nn.sigmoid(i_pre + bi_ref[...])
    lam = lam_ref[...]
    log_sig = jnp.minimum(lam, 0.0) - jnp.log1p(jnp.exp(-jnp.abs(lam)))
    log_a = r * (LRU_C * log_sig)
    a = jnp.exp(log_a)
    z = jnp.maximum(1.0 - jnp.exp(2.0 * log_a), 0.0)
    mult = jnp.where(z > 0.0, z * lax.rsqrt(z), 0.0)
    u = mult * (ig * conv)

    for c in range(n_chunk):
        cs = slice(c * LANES, (c + 1) * LANES)
        for s in range(SUBLANES):
            rows = slice(s * SEG, (s + 1) * SEG)
            a_sc[c, s * SEG_PITCH:s * SEG_PITCH + SEG, :] = a[rows, cs]
            u_sc[c, s * SEG_PITCH:s * SEG_PITCH + SEG, :] = u[rows, cs]
    seg_row = lax.broadcasted_iota(jnp.int32, (SUBLANES, LANES), 0)
    for c in range(n_chunk):
        hs = ap = None
        for g in range(SEG):
            step = pl.ds(g, SUBLANES, stride=SEG_PITCH)
            a_g = a_sc[c, step, :]
            u_g = u_sc[c, step, :]
            hs = u_g if g == 0 else a_g * hs + u_g
            ap = a_g if g == 0 else ap * a_g
            h_sc[c, step, :] = hs
            ap_sc[c, step, :] = ap
        e, p = hs, ap
        shift = 1
        while shift < SUBLANES:
            keep = seg_row >= shift
            e_sh = jnp.where(keep, pltpu.roll(e, shift, 0), 0.0)
            p_sh = jnp.where(keep, pltpu.roll(p, shift, 0), 1.0)
            e = p * e_sh + e
            p = p * p_sh
            shift *= 2
        h_in = hcar[c]
        seg_end = e + p * h_in
        carry = jnp.where(seg_row >= 1, pltpu.roll(seg_end, 1, 0), h_in)
        hcar[c] = jnp.broadcast_to(seg_end[SUBLANES - 1:SUBLANES, :], (SUBLANES, LANES))
        for g in range(SEG):
            step = pl.ds(g, SUBLANES, stride=SEG_PITCH)
            u_sc[c, step, :] = h_sc[c, step, :] + ap_sc[c, step, :] * carry

    sg = gate * jax.nn.sigmoid(gate)
    for c in range(n_chunk):
        cs = slice(c * LANES, (c + 1) * LANES)
        for s in range(SUBLANES):
            rows = slice(s * SEG, (s + 1) * SEG)
            y = u_sc[c, s * SEG_PITCH:s * SEG_PITCH + SEG, :]
            act_sc[rows, cs] = (y * sg[rows, cs]).astype(BF16)
    o_ref[0] = x + _dot(act_sc[...], wout_s[...])


def _rg_layer(x, g, w_in, conv_w, conv_b, w_r, b_r, w_i, b_i, lam, w_out):
    bsz, s, d = x.shape
    width = w_out.shape[0]
    bw = width // LRU_N_BLOCKS
    n_chunk = width // LANES
    const2 = lambda b, t: (0, 0)
    const3 = lambda b, t: (0, 0, 0)
    once = pl.Buffered(1)
    slab = pltpu.VMEM((n_chunk, SUBLANES * SEG_PITCH, LANES), F32)
    return pl.pallas_call(
        _rg_kernel,
        grid=(bsz, s // TILE),
        in_specs=[
            pl.BlockSpec((1, TILE, d), lambda b, t: (b, t, 0)),
            pl.BlockSpec((1, d), const2),
            pl.BlockSpec((d, 2 * width), const2, pipeline_mode=once),
            pl.BlockSpec((CONV_WIDTH, width), const2),
            pl.BlockSpec((1, width), const2),
            pl.BlockSpec((LRU_N_BLOCKS, bw, bw), const3, pipeline_mode=once),
            pl.BlockSpec((1, width), const2),
            pl.BlockSpec((LRU_N_BLOCKS, bw, bw), const3, pipeline_mode=once),
            pl.BlockSpec((1, width), const2),
            pl.BlockSpec((1, width), const2),
            pl.BlockSpec((width, d), const2, pipeline_mode=once),
        ],
        out_specs=pl.BlockSpec((1, TILE, d), lambda b, t: (b, t, 0)),
        out_shape=jax.ShapeDtypeStruct(x.shape, F32),
        scratch_shapes=[
            pltpu.VMEM((d, 2 * width), BF16),
            pltpu.VMEM((LRU_N_BLOCKS, bw, bw), BF16),
            pltpu.VMEM((LRU_N_BLOCKS, bw, bw), BF16),
            pltpu.VMEM((width, d), BF16),
            pltpu.VMEM((n_chunk, TILE + SUBLANES, LANES), F32),
            slab, slab, slab, slab,
            pltpu.VMEM((TILE, width), BF16),
            pltpu.VMEM((n_chunk, SUBLANES, LANES), F32),
        ],
        compiler_params=pltpu.CompilerParams(
            dimension_semantics=("arbitrary", "arbitrary"), vmem_limit_bytes=VMEM_LIMIT),
        name="rg_layer",
    )(x, g.reshape(1, d), w_in.astype(BF16), conv_w, conv_b.reshape(1, width),
      w_r.astype(BF16), b_r.reshape(1, width), w_i.astype(BF16), b_i.reshape(1, width),
      lam.reshape(1, width), w_out.astype(BF16))


def _at_in_kernel(x_ref, g_ref, win_ref, cos_ref, sa_ref, sb_ref,
                  q_ref, k_ref, v_ref, gate_ref, selm_ref, win_s, mt):
    i = pl.program_id(1)
    aw = N_HEADS * HEAD_DIM

    @pl.when((pl.program_id(0) == 0) & (i == 0))
    def _():
        win_s[...] = win_ref[...]

    @pl.when(i == 0)
    def _():
        mt[...] = jnp.zeros_like(mt)

    x = x_ref[0]
    h = _rms_norm(x, g_ref[...]).astype(BF16)
    proj = _dot(h, win_s[...])

    cos = cos_ref[...]
    sa = sa_ref[...]
    sb = sb_ref[...]

    def rotary(z):
        return (z * cos + pltpu.roll(z, HEAD_DIM - ROT_HALF, 1) * sa
                + pltpu.roll(z, ROT_HALF, 1) * sb)

    qf = jnp.concatenate(
        [rotary(proj[:, hh * HEAD_DIM:(hh + 1) * HEAD_DIM]) for hh in range(N_HEADS)], axis=1)
    kf = jnp.concatenate(
        [rotary(proj[:, aw + hh * HEAD_DIM:aw + (hh + 1) * HEAD_DIM]) for hh in range(N_HEADS)],
        axis=1)

    q_hi, q_lo = _split_bf16(qf)
    m_hi, m_lo = _split_bf16(mt[...])
    score = _dot_nt(q_hi, m_hi) + (_dot_nt(q_hi, m_lo) + _dot_nt(q_lo, m_hi))
    lane = lax.broadcasted_iota(jnp.int32, (TILE, LANES), 1)
    blk = lane // N_HEADS
    valid = blk < i
    score = jnp.where(valid, score, -jnp.inf)
    beaten = jnp.zeros((TILE, LANES), jnp.int32)
    for d in range(1, LANES // N_HEADS):
        other = pltpu.roll(score, d * N_HEADS, 1)
        wins = (other > score) | ((other == score) & (blk >= d))
        beaten = beaten + wins.astype(jnp.int32)
    sel = valid & (beaten < MOBA_TOPK)
    selm_ref[0] = jnp.where(sel, 0.0, MASK_BIAS).astype(BF16)

    kmean = jnp.mean(kf, axis=0, keepdims=True)
    head_of_col = lax.broadcasted_iota(jnp.int32, (N_HEADS, aw), 1) // HEAD_DIM
    head_of_row = lax.broadcasted_iota(jnp.int32, (N_HEADS, aw), 0)
    mt[pl.ds(pl.multiple_of(i * N_HEADS, N_HEADS), N_HEADS), :] = jnp.where(
        head_of_col == head_of_row, jnp.broadcast_to(kmean, (N_HEADS, aw)), 0.0)

    q_ref[0] = (qf * (LOG2_E / math.sqrt(HEAD_DIM))).astype(BF16)
    k_ref[0] = kf.astype(BF16)
    v_ref[0] = proj[:, 2 * aw:3 * aw].astype(BF16)
    gate_ref[0] = proj[:, 3 * aw:]


def _at_in(x, g, w_in, cos_t, sa_t, sb_t):
    bsz, s, d = x.shape
    aw = N_HEADS * HEAD_DIM
    const2 = lambda b, t: (0, 0)
    row_blk = lambda b, t: (b, t, 0)
    return pl.pallas_call(
        _at_in_kernel,
        grid=(bsz, s // TILE),
        in_specs=[
            pl.BlockSpec((1, TILE, d), row_blk),
            pl.BlockSpec((1, d), const2),
            pl.BlockSpec((d, 4 * aw), const2, pipeline_mode=pl.Buffered(1)),
            pl.BlockSpec((TILE, HEAD_DIM), lambda b, t: (t, 0)),
            pl.BlockSpec((TILE, HEAD_DIM), lambda b, t: (t, 0)),
            pl.BlockSpec((TILE, HEAD_DIM), lambda b, t: (t, 0)),
        ],
        out_specs=[
            pl.BlockSpec((1, TILE, aw), row_blk),
            pl.BlockSpec((1, TILE, aw), row_blk),
            pl.BlockSpec((1, TILE, aw), row_blk),
            pl.BlockSpec((1, TILE, aw), row_blk),
            pl.BlockSpec((1, TILE, LANES), row_blk),
        ],
        out_shape=[
            jax.ShapeDtypeStruct((bsz, s, aw), BF16),
            jax.ShapeDtypeStruct((bsz, s, aw), BF16),
            jax.ShapeDtypeStruct((bsz, s, aw), BF16),
            jax.ShapeDtypeStruct((bsz, s, aw), F32),
            jax.ShapeDtypeStruct((bsz, s, LANES), BF16),
        ],
        scratch_shapes=[pltpu.VMEM((d, 4 * aw), BF16), pltpu.VMEM((LANES, aw), F32)],
        compiler_params=pltpu.CompilerParams(
            dimension_semantics=("arbitrary", "arbitrary"), vmem_limit_bytes=VMEM_LIMIT),
        name="at_in",
    )(x, g.reshape(1, d), w_in.astype(BF16), cos_t, sa_t, sb_t)


def _attn_kernel(q_ref, selm_ref, k_ref, v_ref, gate_ref, o_ref, m_sc, acc_sc):
    i = pl.program_id(1)
    own = pl.multiple_of(i * TILE, TILE)
    rows = lax.broadcasted_iota(jnp.int32, (TILE, TILE), 0)
    cols = lax.broadcasted_iota(jnp.int32, (TILE, TILE), 1)
    causal = cols <= rows
    kv_step = KV_BLOCKS_PER_STEP * TILE

    ones_own = jnp.ones((TILE, LANES), BF16)
    for hh in range(N_HEADS):
        cs = slice(hh * HEAD_DIM, (hh + 1) * HEAD_DIM)
        s = _dot_nt(q_ref[0, :, cs], k_ref[0, pl.ds(own, TILE), cs])
        s = jnp.where(causal, s, -jnp.inf)
        m0 = jnp.max(s, axis=1, keepdims=True)
        p = jnp.exp2(s - m0)
        m_sc[hh] = jnp.broadcast_to(m0, (TILE, LANES))
        va = jnp.concatenate([v_ref[0, pl.ds(own, TILE), cs], ones_own], axis=1)
        acc_sc[hh] = _dot(p.astype(BF16), va)

    lane_row = lax.broadcasted_iota(jnp.int32, (1, LANES), 1)
    ones_step = jnp.ones((kv_step, LANES), BF16)

    def body(jj, carry):
        start = pl.multiple_of(jj * kv_step, kv_step)
        selm = selm_ref[0]
        for hh in range(N_HEADS):
            cs = slice(hh * HEAD_DIM, (hh + 1) * HEAD_DIM)
            qa = jnp.concatenate([q_ref[0, :, cs], selm], axis=1)
            onehot = jnp.concatenate([
                jnp.broadcast_to(
                    (lane_row == (jj * KV_BLOCKS_PER_STEP + c) * N_HEADS + hh).astype(BF16),
                    (TILE, LANES))
                for c in range(KV_BLOCKS_PER_STEP)], axis=0)
            ka = jnp.concatenate([k_ref[0, pl.ds(start, kv_step), cs], onehot], axis=1)
            s = _dot_nt(qa, ka)
            m_prev = m_sc[hh]
            m_new = jnp.maximum(m_prev, jnp.max(s, axis=1, keepdims=True))
            alpha = jnp.exp2(m_prev - m_new)
            p = jnp.concatenate(
                [jnp.exp2(s[:, c * LANES:(c + 1) * LANES] - m_new).astype(BF16)
                 for c in range(kv_step // LANES)], axis=1)
            va = jnp.concatenate([v_ref[0, pl.ds(start, kv_step), cs], ones_step], axis=1)
            pv = _dot(p, va)
            acc_sc[hh, :, :HEAD_DIM] = alpha * acc_sc[hh, :, :HEAD_DIM] + pv[:, :HEAD_DIM]
            acc_sc[hh, :, HEAD_DIM:] = alpha * acc_sc[hh, :, HEAD_DIM:] + pv[:, HEAD_DIM:]
            m_sc[hh] = m_new
        return carry

    lax.fori_loop(0, (i + KV_BLOCKS_PER_STEP - 1) // KV_BLOCKS_PER_STEP, body, 0)
    for hh in range(N_HEADS):
        cs = slice(hh * HEAD_DIM, (hh + 1) * HEAD_DIM)
        gate = gate_ref[0, :, cs]
        acc = acc_sc[hh]
        o_ref[0, :, cs] = ((acc[:, :HEAD_DIM] / acc[:, HEAD_DIM:])
                           * (gate * jax.nn.sigmoid(gate))).astype(BF16)


def _attention(q, selm, k, v, gate):
    bsz, s, aw = q.shape
    q_blk = lambda b, i: (b, i, 0)
    kv_blk = lambda b, i: (b, 0, 0)
    return pl.pallas_call(
        _attn_kernel,
        grid=(bsz, s // TILE),
        in_specs=[
            pl.BlockSpec((1, TILE, aw), q_blk),
            pl.BlockSpec((1, TILE, LANES), q_blk),
            pl.BlockSpec((1, s, aw), kv_blk),
            pl.BlockSpec((1, s, aw), kv_blk),
            pl.BlockSpec((1, TILE, aw), q_blk),
        ],
        out_specs=pl.BlockSpec((1, TILE, aw), q_blk),
        out_shape=jax.ShapeDtypeStruct((bsz, s, aw), BF16),
        scratch_shapes=[
            pltpu.VMEM((N_HEADS, TILE, LANES), F32),
            pltpu.VMEM((N_HEADS, TILE, HEAD_DIM + LANES), F32),
        ],
        compiler_params=pltpu.CompilerParams(
            dimension_semantics=("arbitrary", "arbitrary"), vmem_limit_bytes=VMEM_LIMIT),
        name="moba_attn",
    )(q, selm, k, v, gate)


def _at_out_kernel(a_ref, x_ref, w_ref, g_ref, o_ref, w_s, *, final):
    @pl.when((pl.program_id(0) == 0) & (pl.program_id(1) == 0))
    def _():
        w_s[...] = w_ref[...]

    y = x_ref[0] + _dot(a_ref[0], w_s[...])
    if final:
        y = _rms_norm(y, g_ref[...])
    o_ref[0] = y


def _at_out(act, x, w_out, g_final, final):
    bsz, s, d = x.shape
    aw = act.shape[-1]
    row_blk = lambda b, t: (b, t, 0)
    const2 = lambda b, t: (0, 0)
    return pl.pallas_call(
        functools.partial(_at_out_kernel, final=final),
        grid=(bsz, s // TILE),
        in_specs=[
            pl.BlockSpec((1, TILE, aw), row_blk),
            pl.BlockSpec((1, TILE, d), row_blk),
            pl.BlockSpec((aw, d), const2, pipeline_mode=pl.Buffered(1)),
            pl.BlockSpec((1, d), const2),
        ],
        out_specs=pl.BlockSpec((1, TILE, d), row_blk),
        out_shape=jax.ShapeDtypeStruct(x.shape, F32),
        scratch_shapes=[pltpu.VMEM((aw, d), BF16)],
        compiler_params=pltpu.CompilerParams(
            dimension_semantics=("arbitrary", "arbitrary"), vmem_limit_bytes=VMEM_LIMIT),
        name="at_out_final" if final else "at_out",
    )(act, x, w_out.astype(BF16), g_final.reshape(1, d))


def _rotary_tables(s):
    pos = jnp.arange(s, dtype=F32)
    inv_freq = ROPE_THETA ** (-jnp.arange(0, ROT_DIM, 2, dtype=F32) / ROT_DIM)
    ang = pos[:, None] * inv_freq[None, :]
    cos, sin = jnp.cos(ang), jnp.sin(ang)
    zeros = jnp.zeros((s, HEAD_DIM - ROT_DIM), F32)
    half0 = jnp.zeros((s, ROT_HALF), F32)
    cos_t = jnp.concatenate([cos, cos, jnp.ones((s, HEAD_DIM - ROT_DIM), F32)], axis=1)
    sa_t = jnp.concatenate([-sin, half0, zeros], axis=1)
    sb_t = jnp.concatenate([half0, sin, zeros], axis=1)
    return cos_t, sa_t, sb_t


def kernel(x, rg_norm, rg_w_in, rg_conv_w, rg_conv_b, rg_w_rgate, rg_b_rgate, rg_w_igate,
           rg_b_igate, rg_lambda, rg_w_out, at_norm, at_w_in, at_w_out, final_norm):
    bsz, s, d = x.shape
    assert s % TILE == 0 and s // TILE * N_HEADS == LANES
    depth = rg_norm.shape[0] + at_norm.shape[0]
    cos_t, sa_t, sb_t = _rotary_tables(s)
    for layer in range(depth):
        j = layer // 2
        if layer % 2 == 0:
            x = _rg_layer(x, rg_norm[j], rg_w_in[j], rg_conv_w[j], rg_conv_b[j],
                          rg_w_rgate[j], rg_b_rgate[j], rg_w_igate[j], rg_b_igate[j],
                          rg_lambda[j], rg_w_out[j])
        else:
            q, k, v, gate, selm = _at_in(x, at_norm[j], at_w_in[j], cos_t, sa_t, sb_t)
            act = _attention(q, selm, k, v, gate)
            x = _at_out(act, x, at_w_out[j], final_norm, final=(layer == depth - 1))
    return x
```

```python
import functools
import math

import jax
import jax.numpy as jnp
from jax import lax
from jax.experimental import pallas as pl
from jax.experimental.pallas import tpu as pltpu

NORM_EPS = 1e-6
LRU_N_BLOCKS = 4
CONV_WIDTH = 4
LRU_C = 8.0
N_HEADS = 8
HEAD_DIM = 128
ROT_DIM = HEAD_DIM // 4
ROT_HALF = ROT_DIM // 2
ROPE_THETA = 500000.0
MOBA_BLOCK = 256
MOBA_TOPK = 3

SUBLANES = 8
LANES = 128
TILE = MOBA_BLOCK
SEG = TILE // SUBLANES
SEG_PITCH = SEG + 4
OUT_TILE = 4 * TILE
MASK_BIAS = -(2.0 ** 100)
VMEM_LIMIT = 56 * 1024 * 1024
LOG2_E = 1.4426950408889634
KV_BLOCKS_PER_STEP = 2

F32 = jnp.float32
BF16 = jnp.bfloat16


def _rms_norm(x, g):
    ms = jnp.mean(x * x, axis=-1, keepdims=True)
    return x * lax.rsqrt(ms + NORM_EPS) * g


def _dot(a, b):
    return jnp.dot(a, b, preferred_element_type=F32)


def _dot_nt(a, b):
    return lax.dot_general(a, b, (((1,), (1,)), ((), ())), preferred_element_type=F32)


def _split_bf16(x):
    hi = x.astype(BF16)
    lo = (x - hi.astype(F32)).astype(BF16)
    return hi, lo


def _rg_kernel(x_ref, g_ref, win_ref, cw_ref, cb_ref, wr_ref, br_ref, wi_ref, bi_ref,
               lam_ref, wout_ref, o_ref,
               win_s, wr_s, wi_s, wout_s, xb_sc, a_sc, u_sc, ap_sc, h_sc, act_sc, hcar):
    t = pl.program_id(1)
    n_chunk = xb_sc.shape[0]
    width = n_chunk * LANES
    bw = width // LRU_N_BLOCKS

    @pl.when((pl.program_id(0) == 0) & (t == 0))
    def _():
        win_s[...] = win_ref[...]
        wr_s[...] = wr_ref[...]
        wi_s[...] = wi_ref[...]
        wout_s[...] = wout_ref[...]

    @pl.when(t == 0)
    def _():
        xb_sc[:, 0:SUBLANES, :] = jnp.zeros((n_chunk, SUBLANES, LANES), F32)
        hcar[...] = jnp.zeros_like(hcar)

    x = x_ref[0]
    h = _rms_norm(x, g_ref[...]).astype(BF16)
    proj = _dot(h, win_s[...])
    gate = proj[:, width:]

    cw = cw_ref[...]
    cb = cb_ref[...]
    conv_chunks = []
    for c in range(n_chunk):
        cs = slice(c * LANES, (c + 1) * LANES)
        xb_sc[c, SUBLANES:SUBLANES + TILE, :] = proj[:, cs]
        acc = cb[:, cs]
        for tap in range(CONV_WIDTH):
            back = CONV_WIDTH - 1 - tap
            acc = acc + cw[tap:tap + 1, cs] * xb_sc[c, SUBLANES - back:SUBLANES - back + TILE, :]
        conv_chunks.append(acc)
        xb_sc[c, 0:SUBLANES, :] = xb_sc[c, TILE:TILE + SUBLANES, :]
    conv = jnp.concatenate(conv_chunks, axis=1)

    c16 = conv.astype(BF16)
    r_pre = jnp.concatenate(
        [_dot(c16[:, n * bw:(n + 1) * bw], wr_s[n]) for n in range(LRU_N_BLOCKS)], axis=1)
    i_pre = jnp.concatenate(
        [_dot(c16[:, n * bw:(n + 1) * bw], wi_s[n]) for n in range(LRU_N_BLOCKS)], axis=1)
    r = jax.nn.sigmoid(r_pre + br_ref[...])
    ig = jax.nn.sigmoid(i_pre + bi_ref[...])
    lam = lam_ref[...]
    log_sig = jnp.minimum(lam, 0.0) - jnp.log1p(jnp.exp(-jnp.abs(lam)))
    log_a = r * (LRU_C * log_sig)
    a = jnp.exp(log_a)
    z = jnp.maximum(1.0 - jnp.exp(2.0 * log_a), 0.0)
    mult = jnp.where(z > 0.0, z * lax.rsqrt(z), 0.0)
    u = mult * (ig * conv)

    for c in range(n_chunk):
        cs = slice(c * LANES, (c + 1) * LANES)
        for s in range(SUBLANES):
            rows = slice(s * SEG, (s + 1) * SEG)
            a_sc[c, s * SEG_PITCH:s * SEG_PITCH + SEG, :] = a[rows, cs]
            u_sc[c, s * SEG_PITCH:s * SEG_PITCH + SEG, :] = u[rows, cs]
    seg_row = lax.broadcasted_iota(jnp.int32, (SUBLANES, LANES), 0)
    for c in range(n_chunk):
        hs = ap = None
        for g in range(SEG):
            step = pl.ds(g, SUBLANES, stride=SEG_PITCH)
            a_g = a_sc[c, step, :]
            u_g = u_sc[c, step, :]
            hs = u_g if g == 0 else a_g * hs + u_g
            ap = a_g if g == 0 else ap * a_g
            h_sc[c, step, :] = hs
            ap_sc[c, step, :] = ap
        e, p = hs, ap
        shift = 1
        while shift < SUBLANES:
            keep = seg_row >= shift
            e_sh = jnp.where(keep, pltpu.roll(e, shift, 0), 0.0)
            p_sh = jnp.where(keep, pltpu.roll(p, shift, 0), 1.0)
            e = p * e_sh + e
            p = p * p_sh
            shift *= 2
        h_in = hcar[c]
        seg_end = e + p * h_in
        carry = jnp.where(seg_row >= 1, pltpu.roll(seg_end, 1, 0), h_in)
        hcar[c] = jnp.broadcast_to(seg_end[SUBLANES - 1:SUBLANES, :], (SUBLANES, LANES))
        for g in range(SEG):
            step = pl.ds(g, SUBLANES, stride=SEG_PITCH)
            u_sc[c, step, :] = h_sc[c, step, :] + ap_sc[c, step, :] * carry

    sg = gate * jax.nn.sigmoid(gate)
    for c in range(n_chunk):
        cs = slice(c * LANES, (c + 1) * LANES)
        for s in range(SUBLANES):
            rows = slice(s * SEG, (s + 1) * SEG)
            y = u_sc[c, s * SEG_PITCH:s * SEG_PITCH + SEG, :]
            act_sc[rows, cs] = (y * sg[rows, cs]).astype(BF16)
    o_ref[0] = x + _dot(act_sc[...], wout_s[...])


def _rg_layer(x, g, w_in, conv_w, conv_b, w_r, b_r, w_i, b_i, lam, w_out):
    bsz, s, d = x.shape
    width = w_out.shape[0]
    bw = width // LRU_N_BLOCKS
    n_chunk = width // LANES
    const2 = lambda b, t: (0, 0)
    const3 = lambda b, t: (0, 0, 0)
    once = pl.Buffered(1)
    slab = pltpu.VMEM((n_chunk, SUBLANES * SEG_PITCH, LANES), F32)
    return pl.pallas_call(
        _rg_kernel,
        grid=(bsz, s // TILE),
        in_specs=[
            pl.BlockSpec((1, TILE, d), lambda b, t: (b, t, 0)),
            pl.BlockSpec((1, d), const2),
            pl.BlockSpec((d, 2 * width), const2, pipeline_mode=once),
            pl.BlockSpec((CONV_WIDTH, width), const2),
            pl.BlockSpec((1, width), const2),
            pl.BlockSpec((LRU_N_BLOCKS, bw, bw), const3, pipeline_mode=once),
            pl.BlockSpec((1, width), const2),
            pl.BlockSpec((LRU_N_BLOCKS, bw, bw), const3, pipeline_mode=once),
            pl.BlockSpec((1, width), const2),
            pl.BlockSpec((1, width), const2),
            pl.BlockSpec((width, d), const2, pipeline_mode=once),
        ],
        out_specs=pl.BlockSpec((1, TILE, d), lambda b, t: (b, t, 0)),
        out_shape=jax.ShapeDtypeStruct(x.shape, F32),
        scratch_shapes=[
            pltpu.VMEM((d, 2 * width), BF16),
            pltpu.VMEM((LRU_N_BLOCKS, bw, bw), BF16),
            pltpu.VMEM((LRU_N_BLOCKS, bw, bw), BF16),
            pltpu.VMEM((width, d), BF16),
            pltpu.VMEM((n_chunk, TILE + SUBLANES, LANES), F32),
            slab, slab, slab, slab,
            pltpu.VMEM((TILE, width), BF16),
            pltpu.VMEM((n_chunk, SUBLANES, LANES), F32),
        ],
        compiler_params=pltpu.CompilerParams(
            dimension_semantics=("arbitrary", "arbitrary"), vmem_limit_bytes=VMEM_LIMIT),
        name="rg_layer",
    )(x, g.reshape(1, d), w_in.astype(BF16), conv_w, conv_b.reshape(1, width),
      w_r.astype(BF16), b_r.reshape(1, width), w_i.astype(BF16), b_i.reshape(1, width),
      lam.reshape(1, width), w_out.astype(BF16))


def _at_in_kernel(x_ref, g_ref, win_ref, cos_ref, sa_ref, sb_ref,
                  q_ref, k_ref, v_ref, gate_ref, selm_ref, win_s, mt):
    i = pl.program_id(1)
    aw = N_HEADS * HEAD_DIM

    @pl.when((pl.program_id(0) == 0) & (i == 0))
    def _():
        win_s[...] = win_ref[...]

    @pl.when(i == 0)
    def _():
        mt[...] = jnp.zeros_like(mt)

    x = x_ref[0]
    h = _rms_norm(x, g_ref[...]).astype(BF16)
    proj = _dot(h, win_s[...])

    cos = cos_ref[...]
    sa = sa_ref[...]
    sb = sb_ref[...]

    def rotary(z):
        return (z * cos + pltpu.roll(z, HEAD_DIM - ROT_HALF, 1) * sa
                + pltpu.roll(z, ROT_HALF, 1) * sb)

    qf = jnp.concatenate(
        [rotary(proj[:, hh * HEAD_DIM:(hh + 1) * HEAD_DIM]) for hh in range(N_HEADS)], axis=1)
    kf = jnp.concatenate(
        [rotary(proj[:, aw + hh * HEAD_DIM:aw + (hh + 1) * HEAD_DIM]) for hh in range(N_HEADS)],
        axis=1)

    q_hi, q_lo = _split_bf16(qf)
    m_hi, m_lo = _split_bf16(mt[...])
    score = _dot_nt(q_hi, m_hi) + (_dot_nt(q_hi, m_lo) + _dot_nt(q_lo, m_hi))
    lane = lax.broadcasted_iota(jnp.int32, (TILE, LANES), 1)
    blk = lane // N_HEADS
    valid = blk < i
    score = jnp.where(valid, score, -jnp.inf)
    beaten = jnp.zeros((TILE, LANES), jnp.int32)
    for d in range(1, LANES // N_HEADS):
        other = pltpu.roll(score, d * N_HEADS, 1)
        wins = (other > score) | ((other == score) & (blk >= d))
        beaten = beaten + wins.astype(jnp.int32)
    sel = valid & (beaten < MOBA_TOPK)
    selm_ref[0] = jnp.where(sel, 0.0, MASK_BIAS).astype(BF16)

    kmean = jnp.mean(kf, axis=0, keepdims=True)
    head_of_col = lax.broadcasted_iota(jnp.int32, (N_HEADS, aw), 1) // HEAD_DIM
    head_of_row = lax.broadcasted_iota(jnp.int32, (N_HEADS, aw), 0)
    mt[pl.ds(pl.multiple_of(i * N_HEADS, N_HEADS), N_HEADS), :] = jnp.where(
        head_of_col == head_of_row, jnp.broadcast_to(kmean, (N_HEADS, aw)), 0.0)

    q_ref[0] = (qf * (LOG2_E / math.sqrt(HEAD_DIM))).astype(BF16)
    k_ref[0] = kf.astype(BF16)
    v_ref[0] = proj[:, 2 * aw:3 * aw].astype(BF16)
    gate_ref[0] = proj[:, 3 * aw:]


def _at_in(x, g, w_in, cos_t, sa_t, sb_t):
    bsz, s, d = x.shape
    aw = N_HEADS * HEAD_DIM
    const2 = lambda b, t: (0, 0)
    row_blk = lambda b, t: (b, t, 0)
    return pl.pallas_call(
        _at_in_kernel,
        grid=(bsz, s // TILE),
        in_specs=[
            pl.BlockSpec((1, TILE, d), row_blk),
            pl.BlockSpec((1, d), const2),
            pl.BlockSpec((d, 4 * aw), const2, pipeline_mode=pl.Buffered(1)),
            pl.BlockSpec((TILE, HEAD_DIM), lambda b, t: (t, 0)),
            pl.BlockSpec((TILE, HEAD_DIM), lambda b, t: (t, 0)),
            pl.BlockSpec((TILE, HEAD_DIM), lambda b, t: (t, 0)),
        ],
        out_specs=[
            pl.BlockSpec((1, TILE, aw), row_blk),
            pl.BlockSpec((1, TILE, aw), row_blk),
            pl.BlockSpec((1, TILE, aw), row_blk),
            pl.BlockSpec((1, TILE, aw), row_blk),
            pl.BlockSpec((1, TILE, LANES), row_blk),
        ],
        out_shape=[
            jax.ShapeDtypeStruct((bsz, s, aw), BF16),
            jax.ShapeDtypeStruct((bsz, s, aw), BF16),
            jax.ShapeDtypeStruct((bsz, s, aw), BF16),
            jax.ShapeDtypeStruct((bsz, s, aw), F32),
            jax.ShapeDtypeStruct((bsz, s, LANES), BF16),
        ],
        scratch_shapes=[pltpu.VMEM((d, 4 * aw), BF16), pltpu.VMEM((LANES, aw), F32)],
        compiler_params=pltpu.CompilerParams(
            dimension_semantics=("arbitrary", "arbitrary"), vmem_limit_bytes=VMEM_LIMIT),
        name="at_in",
    )(x, g.reshape(1, d), w_in.astype(BF16), cos_t, sa_t, sb_t)


def _attn_kernel(q_ref, selm_ref, k_ref, v_ref, gate_ref, o_ref,
                 s_a, s_b, mx_a, mx_b, m_sc, acc_sc):
    i = pl.program_id(1)
    kv_step = KV_BLOCKS_PER_STEP * TILE
    n_units = i // KV_BLOCKS_PER_STEP + 1
    lane_row = lax.broadcasted_iota(jnp.int32, (1, LANES), 1)
    ones_step = jnp.ones((kv_step, LANES), BF16)

    def scores(u, s_buf, mx_buf, causal, heads=range(N_HEADS)):
        start = pl.multiple_of(u * kv_step, kv_step)
        if causal:
            q_pos = i * TILE + lax.broadcasted_iota(jnp.int32, (TILE, kv_step), 0)
            k_pos = start + lax.broadcasted_iota(jnp.int32, (TILE, kv_step), 1)
            visible = k_pos <= q_pos
        selm = selm_ref[0]
        for hh in heads:
            cs = slice(hh * HEAD_DIM, (hh + 1) * HEAD_DIM)
            qa = jnp.concatenate([q_ref[0, :, cs], selm], axis=1)
            onehot = []
            for c in range(KV_BLOCKS_PER_STEP):
                blk = u * KV_BLOCKS_PER_STEP + c
                hit = (lane_row == blk * N_HEADS + hh) & (blk < i)
                onehot.append(jnp.broadcast_to(hit.astype(BF16), (TILE, LANES)))
            ka = jnp.concatenate(
                [k_ref[0, pl.ds(start, kv_step), cs], jnp.concatenate(onehot, axis=0)], axis=1)
            s = _dot_nt(qa, ka)
            if causal:
                s = jnp.where(visible, s, -jnp.inf)
            s_buf[hh] = s
            mx_buf[hh] = jnp.broadcast_to(jnp.max(s, axis=1, keepdims=True), (TILE, LANES))

    def accumulate(u, s_buf, mx_buf, heads=range(N_HEADS)):
        start = pl.multiple_of(u * kv_step, kv_step)
        for hh in heads:
            cs = slice(hh * HEAD_DIM, (hh + 1) * HEAD_DIM)
            m_prev = m_sc[hh]
            m_new = jnp.maximum(m_prev, mx_buf[hh])
            alpha = jnp.exp2(m_prev - m_new)
            p = jnp.concatenate(
                [jnp.exp2(s_buf[hh, :, c * LANES:(c + 1) * LANES] - m_new).astype(BF16)
                 for c in range(kv_step // LANES)], axis=1)
            va = jnp.concatenate([v_ref[0, pl.ds(start, kv_step), cs], ones_step], axis=1)
            pv = _dot(p, va)
            acc_sc[hh, :, :HEAD_DIM] = alpha * acc_sc[hh, :, :HEAD_DIM] + pv[:, :HEAD_DIM]
            acc_sc[hh, :, HEAD_DIM:] = alpha * acc_sc[hh, :, HEAD_DIM:] + pv[:, HEAD_DIM:]
            m_sc[hh] = m_new

    m_sc[...] = jnp.full(m_sc.shape, -jnp.inf, F32)
    acc_sc[...] = jnp.zeros(acc_sc.shape, F32)
    last = n_units - 1
    scores(last, s_a, mx_a, causal=True)

    def body(v, carry):
        def step(cur, nxt):
            for hh in range(N_HEADS):
                scores(last - v - 1, *nxt, causal=False, heads=(hh,))
                accumulate(last - v, *cur, heads=(hh,))

        @pl.when(v % 2 == 0)
        def _():
            step((s_a, mx_a), (s_b, mx_b))

        @pl.when(v % 2 == 1)
        def _():
            step((s_b, mx_b), (s_a, mx_a))

        return carry

    lax.fori_loop(0, last, body, 0)

    @pl.when(last % 2 == 0)
    def _():
        accumulate(0, s_a, mx_a)

    @pl.when(last % 2 == 1)
    def _():
        accumulate(0, s_b, mx_b)

    for hh in range(N_HEADS):
        cs = slice(hh * HEAD_DIM, (hh + 1) * HEAD_DIM)
        gate = gate_ref[0, :, cs]
        acc = acc_sc[hh]
        o_ref[0, :, cs] = ((acc[:, :HEAD_DIM] / acc[:, HEAD_DIM:])
                           * (gate * jax.nn.sigmoid(gate))).astype(BF16)


def _attention(q, selm, k, v, gate):
    bsz, s, aw = q.shape
    q_blk = lambda b, i: (b, i, 0)
    kv_blk = lambda b, i: (b, 0, 0)
    return pl.pallas_call(
        _attn_kernel,
        grid=(bsz, s // TILE),
        in_specs=[
            pl.BlockSpec((1, TILE, aw), q_blk),
            pl.BlockSpec((1, TILE, LANES), q_blk),
            pl.BlockSpec((1, s, aw), kv_blk),
            pl.BlockSpec((1, s, aw), kv_blk),
            pl.BlockSpec((1, TILE, aw), q_blk),
        ],
        out_specs=pl.BlockSpec((1, TILE, aw), q_blk),
        out_shape=jax.ShapeDtypeStruct((bsz, s, aw), BF16),
        scratch_shapes=[
            pltpu.VMEM((N_HEADS, TILE, KV_BLOCKS_PER_STEP * TILE), F32),
            pltpu.VMEM((N_HEADS, TILE, KV_BLOCKS_PER_STEP * TILE), F32),
            pltpu.VMEM((N_HEADS, TILE, LANES), F32),
            pltpu.VMEM((N_HEADS, TILE, LANES), F32),
            pltpu.VMEM((N_HEADS, TILE, LANES), F32),
            pltpu.VMEM((N_HEADS, TILE, HEAD_DIM + LANES), F32),
        ],
        compiler_params=pltpu.CompilerParams(
            dimension_semantics=("arbitrary", "arbitrary"), vmem_limit_bytes=VMEM_LIMIT),
        name="moba_attn",
    )(q, selm, k, v, gate)


def _at_out_kernel(a_ref, x_ref, w_ref, g_ref, o_ref, w_s, *, final):
    @pl.when((pl.program_id(0) == 0) & (pl.program_id(1) == 0))
    def _():
        w_s[...] = w_ref[...]

    y = x_ref[0] + _dot(a_ref[0], w_s[...])
    if final:
        y = _rms_norm(y, g_ref[...])
    o_ref[0] = y


def _at_out(act, x, w_out, g_final, final):
    bsz, s, d = x.shape
    aw = act.shape[-1]
    row_blk = lambda b, t: (b, t, 0)
    const2 = lambda b, t: (0, 0)
    return pl.pallas_call(
        functools.partial(_at_out_kernel, final=final),
        grid=(bsz, s // OUT_TILE),
        in_specs=[
            pl.BlockSpec((1, OUT_TILE, aw), row_blk),
            pl.BlockSpec((1, OUT_TILE, d), row_blk),
            pl.BlockSpec((aw, d), const2, pipeline_mode=pl.Buffered(1)),
            pl.BlockSpec((1, d), const2),
        ],
        out_specs=pl.BlockSpec((1, OUT_TILE, d), row_blk),
        out_shape=jax.ShapeDtypeStruct(x.shape, F32),
        scratch_shapes=[pltpu.VMEM((aw, d), BF16)],
        compiler_params=pltpu.CompilerParams(
            dimension_semantics=("arbitrary", "arbitrary"), vmem_limit_bytes=VMEM_LIMIT),
        name="at_out_final" if final else "at_out",
    )(act, x, w_out.astype(BF16), g_final.reshape(1, d))


def _rotary_tables(s):
    pos = jnp.arange(s, dtype=F32)
    inv_freq = ROPE_THETA ** (-jnp.arange(0, ROT_DIM, 2, dtype=F32) / ROT_DIM)
    ang = pos[:, None] * inv_freq[None, :]
    cos, sin = jnp.cos(ang), jnp.sin(ang)
    zeros = jnp.zeros((s, HEAD_DIM - ROT_DIM), F32)
    half0 = jnp.zeros((s, ROT_HALF), F32)
    cos_t = jnp.concatenate([cos, cos, jnp.ones((s, HEAD_DIM - ROT_DIM), F32)], axis=1)
    sa_t = jnp.concatenate([-sin, half0, zeros], axis=1)
    sb_t = jnp.concatenate([half0, sin, zeros], axis=1)
    return cos_t, sa_t, sb_t


def kernel(x, rg_norm, rg_w_in, rg_conv_w, rg_conv_b, rg_w_rgate, rg_b_rgate, rg_w_igate,
           rg_b_igate, rg_lambda, rg_w_out, at_norm, at_w_in, at_w_out, final_norm):
    bsz, s, d = x.shape
    assert s % OUT_TILE == 0 and s // TILE * N_HEADS == LANES
    depth = rg_norm.shape[0] + at_norm.shape[0]
    cos_t, sa_t, sb_t = _rotary_tables(s)
    for layer in range(depth):
        j = layer // 2
        if layer % 2 == 0:
            x = _rg_layer(x, rg_norm[j], rg_w_in[j], rg_conv_w[j], rg_conv_b[j],
                          rg_w_rgate[j], rg_b_rgate[j], rg_w_igate[j], rg_b_igate[j],
                          rg_lambda[j], rg_w_out[j])
        else:
            q, k, v, gate, selm = _at_in(x, at_norm[j], at_w_in[j], cos_t, sa_t, sb_t)
            act = _attention(q, selm, k, v, gate)
            x = _at_out(act, x, at_w_out[j], final_norm, final=(layer == depth - 1))
    return x
```
